```python
import jax, jax.numpy as jnp
from jax import lax
import numpy as np

D_MODEL = 2048
BATCH = 8
SEQ = 2048
DEPTH = 4

CHUNK = 64
D_MIX = D_MODEL
HEAD_DIM = 128
W_ATTN = D_MIX // 2
H_ATTN = W_ATTN // HEAD_DIM
W_RET = D_MIX - W_ATTN
H_RET = W_RET // HEAD_DIM
LEFT_CHUNKS = 8
BAND = (LEFT_CHUNKS + 1) * CHUNK
REL_CLIP = 128
N_REL = 2 * REL_CLIP + 1
ROPE_BASE = 10000.0
IN_COLS = 3 * W_ATTN + 4 * W_RET
PEER_HEADS = 8
PEER_DQ = 256
N_KEYS = 128
N_EXPERTS = N_KEYS * N_KEYS
PEER_TOPK = 16
PEER_TOKEN_BLOCK = 128
EPS = 1e-6
NEG_INF = -1e30

kernel_name = "hybrid_chunkattn_retention_peer"


def rms_norm(x, g):
    xf = x.astype(jnp.float32)
    y = xf * lax.rsqrt(jnp.mean(xf * xf, axis=-1, keepdims=True) + EPS)
    return (y * g.astype(jnp.float32)).astype(x.dtype)


def split_heads(t, n_heads):
    b, s, _ = t.shape
    return t.reshape(b, s, n_heads, HEAD_DIM).transpose(0, 2, 1, 3)


def merge_heads(t):
    b, h, s, d = t.shape
    return t.transpose(0, 2, 1, 3).reshape(b, s, h * d)


def rotary(x, pos):
    half = x.shape[-1] // 2
    inv_freq = ROPE_BASE ** (-jnp.arange(half, dtype=jnp.float32) / half)
    ang = pos.astype(jnp.float32)[:, None] * inv_freq[None, :]
    cos, sin = jnp.cos(ang), jnp.sin(ang)
    xf = x.astype(jnp.float32)
    x1, x2 = xf[..., :half], xf[..., half:]
    return jnp.concatenate([x1 * cos - x2 * sin, x2 * cos + x1 * sin], axis=-1).astype(x.dtype)


def chunked_rel_attention(q, k, v, rel_bias):
    b, h, s, dh = q.shape
    n_chunks = s // CHUNK
    pad = LEFT_CHUNKS * CHUNK
    k_pad = jnp.pad(k, ((0, 0), (0, 0), (pad, 0), (0, 0)))
    v_pad = jnp.pad(v, ((0, 0), (0, 0), (pad, 0), (0, 0)))
    q_off = jnp.arange(CHUNK)[:, None] + pad
    k_off = jnp.arange(BAND)[None, :]
    rel = jnp.clip(q_off - k_off, -REL_CLIP, REL_CLIP) + REL_CLIP
    bias = rel_bias[:, rel].astype(jnp.float32)
    scale = dh ** -0.5

    def one_chunk(c):
        start = c * CHUNK
        qc = lax.dynamic_slice_in_dim(q, start, CHUNK, axis=2)
        kb = lax.dynamic_slice_in_dim(k_pad, start, BAND, axis=2)
        vb = lax.dynamic_slice_in_dim(v_pad, start, BAND, axis=2)
        sc = jnp.einsum('bhqd,bhkd->bhqk', qc, kb, preferred_element_type=jnp.float32) * scale + bias
        valid = (start + k_off - pad) >= 0
        sc = jnp.where(valid[None, None], sc, NEG_INF)
        p = jax.nn.softmax(sc, axis=-1)
        return jnp.einsum('bhqk,bhkd->bhqd', p.astype(vb.dtype), vb)

    out = lax.map(one_chunk, jnp.arange(n_chunks))
    return out.transpose(1, 2, 0, 3, 4).reshape(b, h, s, dh)


def retention_chunkwise(q, k, v):
    b, h, s, dh = q.shape
    nc = s // CHUNK
    log_gamma = jnp.log1p(-(2.0 ** (-5.0 - jnp.arange(h, dtype=jnp.float32))))
    pos = jnp.arange(CHUNK, dtype=jnp.float32)
    diff = pos[:, None] - pos[None, :]
    decay_intra = jnp.where(diff >= 0, jnp.exp(jnp.maximum(diff, 0.0) * log_gamma[:, None, None]), 0.0)
    q_decay = jnp.exp((pos + 1.0)[None, :] * log_gamma[:, None])
    k_decay = jnp.exp((CHUNK - 1.0 - pos)[None, :] * log_gamma[:, None])
    chunk_decay = jnp.exp(CHUNK * log_gamma)

    qc = q.astype(jnp.float32).reshape(b, h, nc, CHUNK, dh)
    kc = k.astype(jnp.float32).reshape(b, h, nc, CHUNK, dh) * (dh ** -0.5)
    vc = v.astype(jnp.float32).reshape(b, h, nc, CHUNK, dh)

    scores = jnp.einsum('bhnqd,bhnkd->bhnqk', qc, kc) * decay_intra[None, :, None]
    o_intra = jnp.einsum('bhnqk,bhnkd->bhnqd', scores, vc)

    kv = jnp.einsum('bhnkd,bhnke->bhnde', kc * k_decay[None, :, None, :, None], vc)

    def step(state, kv_n):
        return chunk_decay[None, :, None, None] * state + kv_n, state

    state0 = jnp.zeros((b, h, dh, dh), jnp.float32)
    _, states = lax.scan(step, state0, jnp.moveaxis(kv, 2, 0))
    o_cross = jnp.einsum('bhnqd,nbhde->bhnqe', qc * q_decay[None, :, None, :, None], states)
    return (o_intra + o_cross).reshape(b, h, s, dh)


def peer_ffn(h, wq, sub_keys, u_tab, v_tab):
    t = h.shape[0]
    q = (h @ wq).reshape(t, PEER_HEADS, 2, PEER_DQ // 2)
    s = jnp.einsum('thpd,hpkd->thpk', q, sub_keys, preferred_element_type=jnp.float32)
    s_top, i_top = lax.top_k(s, PEER_TOPK)
    cand = (s_top[:, :, 0, :, None] + s_top[:, :, 1, None, :]).reshape(t, PEER_HEADS, PEER_TOPK * PEER_TOPK)
    g_top, c_idx = lax.top_k(cand, PEER_TOPK)
    i1 = jnp.take_along_axis(i_top[:, :, 0], c_idx // PEER_TOPK, axis=-1)
    i2 = jnp.take_along_axis(i_top[:, :, 1], c_idx % PEER_TOPK, axis=-1)
    n_sel = PEER_HEADS * PEER_TOPK
    experts = (i1 * N_KEYS + i2).reshape(t, n_sel)
    gates = jax.nn.softmax(g_top, axis=-1).reshape(t, n_sel)

    nb = t // PEER_TOKEN_BLOCK

    def block(args):
        hb, eb, gb = args
        u = u_tab[eb]
        v = v_tab[eb]
        a = jax.nn.gelu(jnp.einsum('td,ted->te', hb, u, preferred_element_type=jnp.float32), approximate=False)
        return jnp.einsum('te,ted->td', (gb * a).astype(v.dtype), v)

    out = lax.map(block, (h.reshape(nb, PEER_TOKEN_BLOCK, -1),
                          experts.reshape(nb, PEER_TOKEN_BLOCK, n_sel),
                          gates.reshape(nb, PEER_TOKEN_BLOCK, n_sel)))
    return out.reshape(t, -1)


def hybrid_layer(x, norm1_g, w_in, qa_g, ka_g, rel_bias, attn_g, ret_g, w_out,
                 norm2_g, peer_wq, peer_keys, peer_u, peer_v):
    b, s, d = x.shape
    hn = rms_norm(x, norm1_g)
    proj = hn @ w_in
    qa, ka, va, qr, kr, vr, gr = jnp.split(
        proj, [W_ATTN, 2 * W_ATTN, 3 * W_ATTN, 3 * W_ATTN + W_RET,
               3 * W_ATTN + 2 * W_RET, 3 * W_ATTN + 3 * W_RET], axis=-1)

    qa = rms_norm(split_heads(qa, H_ATTN), qa_g)
    ka = rms_norm(split_heads(ka, H_ATTN), ka_g)
    oa = chunked_rel_attention(qa, ka, split_heads(va, H_ATTN), rel_bias)
    oa = merge_heads(rms_norm(oa, attn_g.reshape(H_ATTN, 1, HEAD_DIM)))

    pos = jnp.arange(s)
    qr = rotary(split_heads(qr, H_RET), pos)
    kr = rotary(split_heads(kr, H_RET), pos)
    orr = retention_chunkwise(qr, kr, split_heads(vr, H_RET))
    orr = merge_heads(rms_norm(orr, ret_g.reshape(H_RET, 1, HEAD_DIM))).astype(x.dtype)
    orr = jax.nn.silu(gr) * orr

    x = x + jnp.concatenate([oa, orr], axis=-1) @ w_out

    h2 = rms_norm(x, norm2_g).reshape(b * s, d)
    x = x + peer_ffn(h2, peer_wq, peer_keys, peer_u, peer_v).reshape(b, s, d)
    return x


def setup_inputs(seed: int = 0) -> dict:
    key = jax.random.key(seed)
    ks = jax.random.split(key, 15)

    def nrm(k, shape, scale):
        return jax.random.normal(k, shape, jnp.float32) * scale

    def gain(k, shape):
        return 1.0 + 0.01 * jax.random.normal(k, shape, jnp.float32)

    return {
        "x": nrm(ks[0], (BATCH, SEQ, D_MODEL), 1.0),
        "norm1_g": gain(ks[1], (DEPTH, D_MODEL)),
        "w_in": nrm(ks[2], (DEPTH, D_MODEL, IN_COLS), D_MODEL ** -0.5),
        "qa_norm_g": gain(ks[3], (DEPTH, HEAD_DIM)),
        "ka_norm_g": gain(ks[4], (DEPTH, HEAD_DIM)),
        "rel_bias": nrm(ks[5], (DEPTH, H_ATTN, N_REL), 0.1),
        "attn_out_g": gain(ks[6], (DEPTH, W_ATTN)),
        "ret_out_g": gain(ks[7], (DEPTH, W_RET)),
        "w_out": nrm(ks[8], (DEPTH, D_MIX, D_MODEL), D_MIX ** -0.5),
        "norm2_g": gain(ks[9], (DEPTH, D_MODEL)),
        "peer_wq": nrm(ks[10], (DEPTH, D_MODEL, PEER_HEADS * PEER_DQ), D_MODEL ** -0.5),
        "peer_subkeys": nrm(ks[11], (DEPTH, PEER_HEADS, 2, N_KEYS, PEER_DQ // 2), (PEER_DQ // 2) ** -0.5),
        "peer_u": nrm(ks[12], (DEPTH, N_EXPERTS, D_MODEL), D_MODEL ** -0.5),
        "peer_v": nrm(ks[13], (DEPTH, N_EXPERTS, D_MODEL), 0.5 * PEER_HEADS ** -0.5),
    }


def reference(x, norm1_g, w_in, qa_norm_g, ka_norm_g, rel_bias, attn_out_g, ret_out_g,
              w_out, norm2_g, peer_wq, peer_subkeys, peer_u, peer_v):
    for l in range(DEPTH):
        x = hybrid_layer(x, norm1_g[l], w_in[l], qa_norm_g[l], ka_norm_g[l], rel_bias[l],
                         attn_out_g[l], ret_out_g[l], w_out[l], norm2_g[l],
                         peer_wq[l], peer_subkeys[l], peer_u[l], peer_v[l])
    return x
```

```python
import functools
import math

import jax
import jax.numpy as jnp
import numpy as np
from jax import lax
from jax.experimental import pallas as pl
from jax.experimental.pallas import tpu as pltpu

F32 = jnp.float32
BF16 = jnp.bfloat16

HEAD_DIM = 128
CHUNK = 64
LEFT_CHUNKS = 8
REL_CLIP = 128
ROPE_BASE = 10000.0
PEER_HEADS = 8
N_KEYS = 128
PEER_TOPK = 16
EPS = 1e-6
NEG_INF = -1e30
ATTN_SCALE = HEAD_DIM ** -0.5

VMEM_LIMIT_BYTES = 56 * 1024 * 1024

Q_TILE = 4 * CHUNK
KEY_WINDOW = Q_TILE + LEFT_CHUNKS * CHUNK
RET_GROUP = 4
TOP_ROWS = 24


def _cparams(*sem):
    return pltpu.CompilerParams(dimension_semantics=sem, vmem_limit_bytes=VMEM_LIMIT_BYTES)


def _rms(x, axis=-1):
    return x * lax.rsqrt(jnp.mean(x * x, axis=axis, keepdims=True) + EPS)


def _inproj_kernel(x_ref, g1_ref, w_ref, qg_ref, kg_ref, cos_ref, sin_ref, o_ref, hn_ref):
    j = pl.program_id(1)

    @pl.when(j == 0)
    def _():
        hn_ref[...] = (_rms(x_ref[...]) * g1_ref[...]).astype(BF16)

    y = jnp.dot(hn_ref[...], w_ref[...], preferred_element_type=F32)
    n_heads = y.shape[1] // HEAD_DIM

    def heads(fn):
        for h in range(n_heads):
            sl = slice(h * HEAD_DIM, (h + 1) * HEAD_DIM)
            o_ref[:, sl] = fn(y[:, sl]).astype(BF16)

    def rotary(yh):
        return yh * cos_ref[...] + pltpu.roll(yh, HEAD_DIM // 2, 1) * sin_ref[...]

    @pl.when(j == 0)
    def _():
        heads(lambda yh: _rms(yh) * qg_ref[...])

    @pl.when(j == 1)
    def _():
        heads(lambda yh: _rms(yh) * kg_ref[...])

    @pl.when(jnp.logical_or(j == 3, j == 4))
    def _():
        heads(rotary)

    @pl.when(jnp.logical_or(j == 2, j >= 5))
    def _():
        o_ref[...] = y.astype(BF16)


def _inproj(x2d, g1, w_in, qg, kg, cos_t, sin_t, *, seq, tm):
    t, d = x2d.shape
    n_groups = w_in.shape[1] // 1024
    pos_blocks = seq // tm
    return pl.pallas_call(
        _inproj_kernel,
        grid=(t // tm, n_groups),
        in_specs=[
            pl.BlockSpec((tm, d), lambda i, j: (i, 0)),
            pl.BlockSpec((1, d), lambda i, j: (0, 0)),
            pl.BlockSpec((d, 1024), lambda i, j: (0, j)),
            pl.BlockSpec((1, HEAD_DIM), lambda i, j: (0, 0)),
            pl.BlockSpec((1, HEAD_DIM), lambda i, j: (0, 0)),
            pl.BlockSpec((tm, HEAD_DIM), lambda i, j: (i % pos_blocks, 0)),
            pl.BlockSpec((tm, HEAD_DIM), lambda i, j: (i % pos_blocks, 0)),
        ],
        out_specs=pl.BlockSpec((None, tm, 1024), lambda i, j: (j, i, 0)),
        out_shape=jax.ShapeDtypeStruct((n_groups, t, 1024), BF16),
        scratch_shapes=[pltpu.VMEM((tm, d), BF16)],
        compiler_params=_cparams("parallel", "arbitrary"),
        name="inproj",
    )(x2d, g1, w_in, qg, kg, cos_t, sin_t)


def _attn_kernel(q_ref, k_ref, v_ref, b_ref, g_ref, o_ref):
    t = pl.program_id(2)
    ks = pl.multiple_of(jnp.maximum(t * Q_TILE - LEFT_CHUNKS * CHUNK, 0), Q_TILE)
    kb = k_ref[pl.ds(ks, KEY_WINDOW), :]
    vb = v_ref[pl.ds(ks, KEY_WINDOW), :]
    sc = lax.dot_general(q_ref[...], kb, (((1,), (1,)), ((), ())), preferred_element_type=F32)
    sc = sc * ATTN_SCALE + b_ref[...]
    m = jnp.max(sc, axis=-1, keepdims=True)
    e = jnp.exp(sc - m)
    l = jnp.sum(e, axis=-1, keepdims=True)
    o = jnp.dot(e.astype(BF16), vb, preferred_element_type=F32) / l
    o_ref[...] = (_rms(o) * g_ref[...]).astype(BF16)


def _attn(proj, bias, attn_g, *, batch, seq):
    n_heads = attn_g.shape[1] // HEAD_DIM
    p4 = proj.reshape(proj.shape[0], batch, seq, 1024)
    return pl.pallas_call(
        _attn_kernel,
        grid=(n_heads, batch, seq // Q_TILE),
        in_specs=[
            pl.BlockSpec((None, None, Q_TILE, HEAD_DIM), lambda h, b, t: (0, b, t, h)),
            pl.BlockSpec((None, None, seq, HEAD_DIM), lambda h, b, t: (1, b, 0, h)),
            pl.BlockSpec((None, None, seq, HEAD_DIM), lambda h, b, t: (2, b, 0, h)),
            pl.BlockSpec((None, None, Q_TILE, KEY_WINDOW),
                         lambda h, b, t: (jnp.minimum(t, 2), h, 0, 0)),
            pl.BlockSpec((1, HEAD_DIM), lambda h, b, t: (0, h)),
        ],
        out_specs=pl.BlockSpec((None, Q_TILE, HEAD_DIM), lambda h, b, t: (b, t, h)),
        out_shape=jax.ShapeDtypeStruct((batch, seq, n_heads * HEAD_DIM), BF16),
        compiler_params=_cparams("parallel", "parallel", "arbitrary"),
        name="attn",
    )(p4, p4, p4, bias, attn_g)


def _attn_bias(rel_bias):
    i = np.arange(Q_TILE)[:, None]
    j = np.arange(KEY_WINDOW)[None, :]
    out = []
    for off in (0, Q_TILE, 2 * Q_TILE):
        rel = np.clip(i - j + off, -REL_CLIP, REL_CLIP) + REL_CLIP
        qc, kc = i // CHUNK, j // CHUNK
        valid = (kc >= qc - LEFT_CHUNKS + off // CHUNK) & (kc <= qc + off // CHUNK)
        out.append(jnp.where(valid[None], rel_bias[:, rel].astype(F32), NEG_INF))
    return jnp.stack(out)


def _ret_kernel(q_ref, k_ref, v_ref, gate_ref, dm_ref, qd_ref, kd_ref, cd_ref, g_ref, o_ref):
    rows = RET_GROUP * CHUNK
    nt = (((1,), (1,)), ((), ()))
    tn = (((0,), (0,)), ((), ()))

    def group(gi, state):
        r0 = pl.multiple_of(gi * rows, rows)
        q = q_ref[pl.ds(r0, rows), :]
        v = v_ref[pl.ds(r0, rows), :]
        kf = k_ref[pl.ds(r0, rows), :].astype(F32) * ATTN_SCALE
        sc = lax.dot_general(q, kf.astype(BF16), nt, preferred_element_type=F32) * dm_ref[...]
        o_intra = jnp.dot(sc.astype(BF16), v, preferred_element_type=F32)
        kdec = (kf * kd_ref[...]).astype(BF16)
        qdec = (q.astype(F32) * qd_ref[...]).astype(BF16)
        for c in range(RET_GROUP):
            sl = slice(c * CHUNK, (c + 1) * CHUNK)
            o = o_intra[sl] + jnp.dot(qdec[sl], state.astype(BF16), preferred_element_type=F32)
            o = _rms(o) * g_ref[...]
            gate = gate_ref[pl.ds(r0 + c * CHUNK, CHUNK), :].astype(F32)
            o_ref[pl.ds(r0 + c * CHUNK, CHUNK), :] = (gate * jax.nn.sigmoid(gate) * o).astype(BF16)
            kv = lax.dot_general(kdec[sl], v[sl], tn, preferred_element_type=F32)
            state = cd_ref[...] * state + kv
        return state

    n_groups = q_ref.shape[0] // rows
    lax.fori_loop(0, n_groups, group, jnp.zeros((HEAD_DIM, HEAD_DIM), F32))


def _retention(proj, dm, qd, kd, cd, ret_g, *, batch, seq):
    n_heads = ret_g.shape[1] // HEAD_DIM
    p4 = proj.reshape(proj.shape[0], batch, seq, 1024)
    rows = RET_GROUP * CHUNK

    def qkv(g):
        return pl.BlockSpec((None, None, seq, HEAD_DIM), lambda b, h: (g, b, 0, h))

    return pl.pallas_call(
        _ret_kernel,
        grid=(batch, n_heads),
        in_specs=[
            qkv(3), qkv(4), qkv(5), qkv(6),
            pl.BlockSpec((None, rows, rows), lambda b, h: (h, 0, 0)),
            pl.BlockSpec((None, rows, HEAD_DIM), lambda b, h: (h, 0, 0)),
            pl.BlockSpec((None, rows, HEAD_DIM), lambda b, h: (h, 0, 0)),
            pl.BlockSpec((None, 1, HEAD_DIM), lambda b, h: (h, 0, 0)),
            pl.BlockSpec((1, HEAD_DIM), lambda b, h: (0, h)),
        ],
        out_specs=pl.BlockSpec((None, seq, HEAD_DIM), lambda b, h: (b, 0, h)),
        out_shape=jax.ShapeDtypeStruct((batch, seq, n_heads * HEAD_DIM), BF16),
        compiler_params=_cparams("parallel", "parallel"),
        name="retention",
    )(p4, p4, p4, p4, dm, qd, kd, cd, ret_g)


def _ret_consts(n_heads):
    log_gamma = jnp.log1p(-(2.0 ** (-5.0 - jnp.arange(n_heads, dtype=F32))))
    pos = jnp.arange(CHUNK, dtype=F32)
    diff = pos[:, None] - pos[None, :]
    intra = jnp.where(diff >= 0, jnp.exp(jnp.maximum(diff, 0.0) * log_gamma[:, None, None]), 0.0)
    rows = RET_GROUP * CHUNK
    same = (np.arange(rows)[:, None] // CHUNK) == (np.arange(rows)[None, :] // CHUNK)
    dm = jnp.where(same[None], jnp.tile(intra, (1, RET_GROUP, RET_GROUP)), 0.0)
    q_decay = jnp.exp((pos + 1.0)[None, :] * log_gamma[:, None])
    k_decay = jnp.exp((CHUNK - 1.0 - pos)[None, :] * log_gamma[:, None])
    qd = jnp.broadcast_to(jnp.tile(q_decay, (1, RET_GROUP))[:, :, None], (n_heads, rows, HEAD_DIM))
    kd = jnp.broadcast_to(jnp.tile(k_decay, (1, RET_GROUP))[:, :, None], (n_heads, rows, HEAD_DIM))
    cd = jnp.broadcast_to(jnp.exp(CHUNK * log_gamma)[:, None, None], (n_heads, 1, HEAD_DIM))
    return dm, qd, kd, cd


def _outproj_kernel(x_ref, oa_ref, or_ref, wa_ref, wr_ref, g2_ref, xo_ref, h2_ref):
    y = x_ref[...] + jnp.dot(oa_ref[...], wa_ref[...], preferred_element_type=F32)
    y = y + jnp.dot(or_ref[...], wr_ref[...], preferred_element_type=F32)
    xo_ref[...] = y
    h2_ref[...] = (_rms(y) * g2_ref[...]).astype(BF16)


def _outproj(x2d, oa, orr, w_out, g2, *, tm):
    t, d = x2d.shape
    wa_cols = oa.shape[1]
    const = dict(pipeline_mode=pl.Buffered(1))
    return pl.pallas_call(
        _outproj_kernel,
        grid=(t // tm,),
        in_specs=[
            pl.BlockSpec((tm, d), lambda i: (i, 0)),
            pl.BlockSpec((tm, wa_cols), lambda i: (i, 0)),
            pl.BlockSpec((tm, orr.shape[1]), lambda i: (i, 0)),
            pl.BlockSpec((wa_cols, d), lambda i: (0, 0), **const),
            pl.BlockSpec((orr.shape[1], d), lambda i: (1, 0), **const),
            pl.BlockSpec((1, d), lambda i: (0, 0)),
        ],
        out_specs=[pl.BlockSpec((tm, d), lambda i: (i, 0)),
                   pl.BlockSpec((tm, d), lambda i: (i, 0))],
        out_shape=[jax.ShapeDtypeStruct((t, d), F32), jax.ShapeDtypeStruct((t, d), BF16)],
        compiler_params=_cparams("parallel"),
        name="outproj",
    )(x2d, oa, orr, w_out, w_out, g2)


def _route_kernel(h_ref, wq_ref, sk_ref, e1_ref, thr_ref, s2_ref, e2_ref, top_ref):
    nt = (((1,), (1,)), ((), ()))
    k = PEER_TOPK
    half = k // 2
    q = jnp.dot(h_ref[...], wq_ref[...], preferred_element_type=F32).astype(BF16)
    top_ref[:, k:, :] = jnp.full((2, TOP_ROWS - k, top_ref.shape[2]), NEG_INF, F32)

    def sorted_top(s, side):
        for r in range(k + 1):
            m = jnp.max(s, axis=0, keepdims=True)
            top_ref[side, r:r + 1, :] = m
            s = jnp.where(s == m, NEG_INF, s)

    for hp in range(PEER_HEADS):
        base = hp * 2 * N_KEYS
        s1 = lax.dot_general(sk_ref[hp, 0], q[:, base:base + N_KEYS], nt,
                             preferred_element_type=F32)
        s2 = lax.dot_general(sk_ref[hp, 1], q[:, base + N_KEYS:base + 2 * N_KEYS], nt,
                             preferred_element_type=F32)
        sorted_top(s1, 0)
        sorted_top(s2, 1)
        a = top_ref[0]
        b = top_ref[1]
        cands = [a[0:1] + b] + [a[i:i + 1] + b[0:half] for i in range(1, k + 1)]
        cmax = a[0:1] + b[0:1]

        def colmax(xs):
            m = jnp.max(xs[0], axis=0, keepdims=True)
            for x in xs[1:]:
                m = jnp.maximum(m, jnp.max(x, axis=0, keepdims=True))
            return m

        work = list(cands)
        t16 = None
        for r in range(k + 1):
            m = colmax(work)
            if r == k - 1:
                t16 = m
            if r < k:
                work = [jnp.where(x == m, NEG_INF, x) for x in work]
        theta = 0.5 * (t16 + m)
        z = None
        for x in cands:
            part = jnp.sum(jnp.where(x >= theta, jnp.exp(x - cmax), 0.0), axis=0, keepdims=True)
            z = part if z is None else z + part
        e1_ref[hp] = jnp.exp(s1 - a[0:1])
        thr_ref[hp] = theta - s1
        s2_ref[hp] = s2
        e2_ref[hp] = jnp.exp(s2 - b[0:1]) / z


def _route(h2, wq, sk, *, tm):
    t, d = h2.shape
    const = dict(pipeline_mode=pl.Buffered(1))
    o_spec = pl.BlockSpec((PEER_HEADS, N_KEYS, tm), lambda i: (0, 0, i))
    o_shape = jax.ShapeDtypeStruct((PEER_HEADS, N_KEYS, t), F32)
    return pl.pallas_call(
        _route_kernel,
        grid=(t // tm,),
        in_specs=[
            pl.BlockSpec((tm, d), lambda i: (i, 0)),
            pl.BlockSpec(wq.shape, lambda i: (0, 0), **const),
            pl.BlockSpec(sk.shape, lambda i: (0, 0, 0, 0), **const),
        ],
        out_specs=[o_spec] * 4,
        out_shape=[o_shape] * 4,
        scratch_shapes=[pltpu.VMEM((2, TOP_ROWS, tm), F32)],
        compiler_params=_cparams("parallel"),
        name="route",
    )(h2, wq, sk)


def _peer_kernel(x_ref, h_ref, u_ref, v_ref, e1_ref, thr_ref, s2_ref, e2_ref, o_ref, p_ref):
    j = pl.program_id(1)
    rows_k1 = u_ref.shape[0] // N_KEYS

    @pl.when(j == 0)
    def _():
        o_ref[...] = x_ref[...]

    at = lax.dot_general(u_ref[...], h_ref[...], (((1,), (1,)), ((), ())),
                         preferred_element_type=F32)
    for r in range(rows_k1):
        sl = slice(r * N_KEYS, (r + 1) * N_KEYS)
        k1 = j * rows_k1 + r
        w = None
        for hp in range(PEER_HEADS):
            c = e1_ref[hp, pl.ds(k1, 1), :]
            th = thr_ref[hp, pl.ds(k1, 1), :]
            term = c * jnp.where(s2_ref[hp] >= th, e2_ref[hp], 0.0)
            w = term if w is None else w + term
        a = at[sl]
        gelu = 0.5 * a * (1.0 + lax.erf(a * (1.0 / math.sqrt(2.0))))
        p_ref[sl, :] = (w * gelu).astype(BF16)
    o_ref[...] += lax.dot_general(p_ref[...], v_ref[...], (((0,), (0,)), ((), ())),
                                  preferred_element_type=F32)


def _peer(x2d, h2, u, v, e1, thr, s2, e2, *, tb, eb):
    t, d = x2d.shape
    n_exp = u.shape[0]
    r_spec = pl.BlockSpec((PEER_HEADS, N_KEYS, tb), lambda i, j: (0, 0, i))
    return pl.pallas_call(
        _peer_kernel,
        grid=(t // tb, n_exp // eb),
        in_specs=[
            pl.BlockSpec((tb, d), lambda i, j: (i, 0)),
            pl.BlockSpec((tb, d), lambda i, j: (i, 0)),
            pl.BlockSpec((eb, d), lambda i, j: (j, 0)),
            pl.BlockSpec((eb, d), lambda i, j: (j, 0)),
            r_spec, r_spec, r_spec, r_spec,
        ],
        out_specs=pl.BlockSpec((tb, d), lambda i, j: (i, 0)),
        out_shape=jax.ShapeDtypeStruct((t, d), F32),
        scratch_shapes=[pltpu.VMEM((eb, tb), BF16)],
        compiler_params=_cparams("parallel", "arbitrary"),
        name="peer",
    )(x2d, h2, u, v, e1, thr, s2, e2)


def _rope_tables(seq):
    half = HEAD_DIM // 2
    inv_freq = ROPE_BASE ** (-jnp.arange(half, dtype=F32) / half)
    ang = jnp.arange(seq, dtype=F32)[:, None] * inv_freq[None, :]
    cos, sin = jnp.cos(ang), jnp.sin(ang)
    return jnp.concatenate([cos, cos], -1), jnp.concatenate([-sin, sin], -1)


def _layer(x2d, p, consts, *, batch, seq, tiles):
    cos_t, sin_t, dm, qd, kd, cd = consts
    proj = _inproj(x2d, p["g1"], p["w_in"], p["qg"], p["kg"], cos_t, sin_t,
                   seq=seq, tm=tiles["inproj_tm"])
    oa = _attn(proj, _attn_bias(p["rel_bias"]), p["attn_g"], batch=batch, seq=seq)
    orr = _retention(proj, dm, qd, kd, cd, p["ret_g"], batch=batch, seq=seq)
    t = x2d.shape[0]
    x2d, h2 = _outproj(x2d, oa.reshape(t, -1), orr.reshape(t, -1), p["w_out"], p["g2"],
                       tm=tiles["outproj_tm"])
    e1, thr, s2, e2 = _route(h2, p["wq"], p["sk"], tm=tiles["route_tm"])
    return _peer(x2d, h2, p["u"], p["v"], e1, thr, s2, e2, tb=tiles["peer_tb"], eb=tiles["peer_eb"])


def _tiles(t, seq):
    return dict(inproj_tm=min(1024, seq), outproj_tm=min(512, t), route_tm=min(256, t),
                peer_tb=min(512, t), peer_eb=512)


def kernel(x, norm1_g, w_in, qa_norm_g, ka_norm_g, rel_bias, attn_out_g, ret_out_g, w_out, norm2_g,
           peer_wq, peer_subkeys, peer_u, peer_v):
    batch, seq, d = x.shape
    depth = w_in.shape[0]
    n_ret_heads = ret_out_g.shape[1] // HEAD_DIM
    consts = _rope_tables(seq) + _ret_consts(n_ret_heads)
    w_in_b, w_out_b, wq_b = w_in.astype(BF16), w_out.astype(BF16), peer_wq.astype(BF16)
    sk_b, u_b, v_b = peer_subkeys.astype(BF16), peer_u.astype(BF16), peer_v.astype(BF16)
    x2d = x.reshape(batch * seq, d)
    tiles = _tiles(batch * seq, seq)
    for l in range(depth):
        p = dict(g1=norm1_g[l][None], w_in=w_in_b[l], qg=qa_norm_g[l][None], kg=ka_norm_g[l][None],
                 rel_bias=rel_bias[l], attn_g=attn_out_g[l][None], ret_g=ret_out_g[l][None],
                 w_out=w_out_b[l], g2=norm2_g[l][None], wq=wq_b[l], sk=sk_b[l], u=u_b[l], v=v_b[l])
        x2d = _layer(x2d, p, consts, batch=batch, seq=seq, tiles=tiles)
    return x2d.reshape(batch, seq, d)
```
